```python
import jax, jax.numpy as jnp
from jax import lax
import numpy as np

D_MODEL = 1024
BATCH = 1
SEQ = 16384
DEPTH = 4
DEC_BATCH = 16
DEC_SEQ = 32
PAST_LEN = 1024

CHUNK = 64
D_A = D_MODEL // 2
D_B = D_MODEL // 2
D_C = D_MODEL
CONV_A_WIDTH = 31
CONV_C_WIDTH = 3
POOL_WINDOWS = (2, 4, 8, 16)
N_POOL_GROUPS = len(POOL_WINDOWS)
POOL_GROUP = D_B // N_POOL_GROUPS
POOL_HIST = max(POOL_WINDOWS) - 1
D_FF = ((8 * D_MODEL // 3 + 127) // 128) * 128
N_EVEN = (DEPTH + 1) // 2
N_ODD = DEPTH // 2
N_MOD = 9
FFN_RES = 0.5
EPS = 1e-6

kernel_name = "hybrid_streaming_conformer_pool_shortconv_step"


def rmsnorm(x, g):
    xf = x.astype(jnp.float32)
    y = xf * lax.rsqrt(jnp.mean(xf * xf, axis=-1, keepdims=True) + EPS)
    return (y * g.astype(jnp.float32)).astype(x.dtype)


def layernorm(x, g, b):
    xf = x.astype(jnp.float32)
    mu = jnp.mean(xf, axis=-1, keepdims=True)
    xc = xf - mu
    y = xc * lax.rsqrt(jnp.mean(xc * xc, axis=-1, keepdims=True) + EPS)
    return (y * g.astype(jnp.float32) + b.astype(jnp.float32)).astype(x.dtype)


def modulate(h, shift, scale):
    return h * (1 + scale[:, None, :]) + shift[:, None, :]


def causal_dwconv(xpad, w):
    C = xpad.shape[-1]
    return lax.conv_general_dilated(
        xpad, w[:, None, :].astype(xpad.dtype), window_strides=(1,), padding='VALID',
        dimension_numbers=('NWC', 'WIO', 'NWC'), feature_group_count=C)


def swiglu(h, w_gu, w_down):
    g, u = jnp.split(h @ w_gu, 2, axis=-1)
    return (jax.nn.silu(g) * u) @ w_down


def multiscale_pool(u, prev, start):
    L = u.shape[1]
    z = jnp.concatenate([prev, u], axis=1).astype(jnp.float32)
    cs = jnp.pad(jnp.cumsum(z, axis=1), ((0, 0), (1, 0), (0, 0)))
    pos = start + jnp.arange(L)
    end = POOL_HIST + 1
    outs = []
    for g, w in enumerate(POOL_WINDOWS):
        sl = slice(g * POOL_GROUP, (g + 1) * POOL_GROUP)
        s = cs[:, end:end + L, sl] - cs[:, end - w:end - w + L, sl]
        cnt = jnp.minimum(w, pos + 1).astype(jnp.float32)
        outs.append(s / cnt[None, :, None])
    mean = jnp.concatenate(outs, axis=-1)
    return (mean - u.astype(jnp.float32)).astype(u.dtype)


def mixer_ab(h, prev_a, prev_b, start, w_in, conv_w, conv_b, ln_g, ln_b, w_group, scale, w_out):
    B, L, _ = h.shape
    p = h @ w_in
    a_u, a_g, b_u = jnp.split(p, [D_A, 2 * D_A], axis=-1)
    a = a_u * jax.nn.sigmoid(a_g)
    a_pad = jnp.concatenate([prev_a, a], axis=1)
    a = jax.nn.silu(layernorm(causal_dwconv(a_pad, conv_w) + conv_b, ln_g, ln_b))
    d = multiscale_pool(b_u, prev_b, start)
    bo = jnp.einsum('blgc,gcd->blgd', d.reshape(B, L, N_POOL_GROUPS, POOL_GROUP), w_group)
    bo = bo.reshape(B, L, D_B) * scale
    y = jnp.concatenate([a, bo], axis=-1) @ w_out
    new_a = a_pad[:, -(CONV_A_WIDTH - 1):]
    new_b = jnp.concatenate([prev_b, b_u], axis=1)[:, -POOL_HIST:]
    return y, new_a, new_b


def mixer_c(h, prev_c, w_in, conv_w, w_out):
    bg, cg, v = jnp.split(h @ w_in, 3, axis=-1)
    z = jnp.concatenate([prev_c, cg * v], axis=1)
    y = (bg * causal_dwconv(z, conv_w)) @ w_out
    return y, z[:, -(CONV_C_WIDTH - 1):]


def setup_inputs(seed: int = 0) -> dict:
    key = jax.random.key(seed)
    ks = jax.random.split(key, 26)
    nrm = jax.random.normal
    f32 = jnp.float32
    return {
        "x_prompt": nrm(ks[0], (BATCH, SEQ, D_MODEL), f32),
        "x_sample": nrm(ks[1], (DEC_BATCH, DEC_SEQ, D_MODEL), f32),
        "state_conv_a": 0.5 * nrm(ks[2], (N_EVEN, DEC_BATCH, CONV_A_WIDTH - 1, D_A), f32),
        "state_pool_b": nrm(ks[3], (N_EVEN, DEC_BATCH, POOL_HIST, D_B), f32),
        "state_conv_c": 0.5 * nrm(ks[4], (N_ODD, DEC_BATCH, CONV_C_WIDTH - 1, D_C), f32),
        "c_prompt": nrm(ks[5], (BATCH, D_MODEL), f32),
        "c_sample": nrm(ks[6], (DEC_BATCH, D_MODEL), f32),
        "ada_w": 0.5 * D_MODEL ** -0.5 * nrm(ks[7], (DEPTH, D_MODEL, N_MOD * D_MODEL), f32),
        "ada_b": 0.01 * nrm(ks[8], (DEPTH, N_MOD * D_MODEL), f32),
        "norm_g": 1.0 + 0.01 * nrm(ks[9], (DEPTH, 3, D_MODEL), f32),
        "ffn_w_gu": D_MODEL ** -0.5 * nrm(ks[10], (DEPTH, 2, D_MODEL, 2 * D_FF), f32),
        "ffn_w_down": D_FF ** -0.5 * nrm(ks[11], (DEPTH, 2, D_FF, D_MODEL), f32),
        "ab_w_in": D_MODEL ** -0.5 * nrm(ks[12], (N_EVEN, D_MODEL, 2 * D_A + D_B), f32),
        "a_conv_w": CONV_A_WIDTH ** -0.5 * nrm(ks[13], (N_EVEN, CONV_A_WIDTH, D_A), f32),
        "a_conv_b": 0.01 * nrm(ks[14], (N_EVEN, D_A), f32),
        "a_ln_g": 1.0 + 0.01 * nrm(ks[15], (N_EVEN, D_A), f32),
        "a_ln_b": 0.01 * nrm(ks[16], (N_EVEN, D_A), f32),
        "b_w_group": POOL_GROUP ** -0.5 * nrm(ks[17], (N_EVEN, N_POOL_GROUPS, POOL_GROUP, POOL_GROUP), f32),
        "b_scale": 1.0 + 0.01 * nrm(ks[18], (N_EVEN, D_B), f32),
        "ab_w_out": (D_A + D_B) ** -0.5 * nrm(ks[19], (N_EVEN, D_A + D_B, D_MODEL), f32),
        "c_w_in": D_MODEL ** -0.5 * nrm(ks[20], (N_ODD, D_MODEL, 3 * D_C), f32),
        "c_conv_w": CONV_C_WIDTH ** -0.5 * nrm(ks[21], (N_ODD, CONV_C_WIDTH, D_C), f32),
        "c_w_out": D_C ** -0.5 * nrm(ks[22], (N_ODD, D_C, D_MODEL), f32),
        "final_g": 1.0 + 0.01 * nrm(ks[23], (D_MODEL,), f32),
    }


def reference(x_prompt, x_sample, state_conv_a, state_pool_b, state_conv_c, c_prompt, c_sample,
              ada_w, ada_b, norm_g, ffn_w_gu, ffn_w_down, ab_w_in, a_conv_w, a_conv_b, a_ln_g,
              a_ln_b, b_w_group, b_scale, ab_w_out, c_w_in, c_conv_w, c_w_out, final_g):

    def run(x, c, prev_a, prev_b, prev_c, start):
        new_a, new_b, new_c = [], [], []
        c_act = jax.nn.silu(c)
        for l in range(DEPTH):
            mod = c_act @ ada_w[l] + ada_b[l]
            sh1, sc1, g1, sh2, sc2, g2, sh3, sc3, g3 = jnp.split(mod, N_MOD, axis=-1)
            h = modulate(rmsnorm(x, norm_g[l, 0]), sh1, sc1)
            x = x + FFN_RES * g1[:, None, :] * swiglu(h, ffn_w_gu[l, 0], ffn_w_down[l, 0])
            h = modulate(rmsnorm(x, norm_g[l, 1]), sh2, sc2)
            if l % 2 == 0:
                e = l // 2
                y, na, nb = mixer_ab(h, prev_a[e], prev_b[e], start, ab_w_in[e], a_conv_w[e],
                                     a_conv_b[e], a_ln_g[e], a_ln_b[e], b_w_group[e],
                                     b_scale[e], ab_w_out[e])
                new_a.append(na)
                new_b.append(nb)
            else:
                o = l // 2
                y, nc = mixer_c(h, prev_c[o], c_w_in[o], c_conv_w[o], c_w_out[o])
                new_c.append(nc)
            x = x + g2[:, None, :] * y
            h = modulate(rmsnorm(x, norm_g[l, 2]), sh3, sc3)
            x = x + FFN_RES * g3[:, None, :] * swiglu(h, ffn_w_gu[l, 1], ffn_w_down[l, 1])
        return rmsnorm(x, final_g), jnp.stack(new_a), jnp.stack(new_b), jnp.stack(new_c)

    nb_p = x_prompt.shape[0]
    dt = x_prompt.dtype
    zero_a = jnp.zeros((N_EVEN, nb_p, CONV_A_WIDTH - 1, D_A), dt)
    zero_b = jnp.zeros((N_EVEN, nb_p, POOL_HIST, D_B), dt)
    zero_c = jnp.zeros((N_ODD, nb_p, CONV_C_WIDTH - 1, D_C), dt)
    y_prompt, pa, pb, pc = run(x_prompt, c_prompt, zero_a, zero_b, zero_c, 0)
    y_sample, sa, sb, sc = run(x_sample, c_sample, state_conv_a, state_pool_b, state_conv_c, PAST_LEN)
    return (y_prompt, y_sample, pa, pb, pc, sa, sb, sc)
```

```python
import functools

import jax
import jax.numpy as jnp
from jax import lax
from jax.experimental import pallas as pl
from jax.experimental.pallas import tpu as pltpu

D = 1024
F = 2816
D_A = 512
D_B = 512
CONV_A = 31
CONV_C = 3
POOL_WINDOWS = (2, 4, 8, 16)
PAST_LEN = 1024
EPS = 1e-6
FFN_RES = 0.5
DEPTH = 4

LANES = 128
TM = 512
NG = 16
GR = TM // NG
HA = 32
HB = 16
HC = 8
FFN_CHUNK = 512
VMEM_LIMIT = 58 * 1024 * 1024


def _dot(a, b):
    return jnp.dot(a, b, preferred_element_type=jnp.float32)


def _load_x(xp_ref, xs_ref, is_s, rows):
    return jnp.where(is_s, xs_ref[rows, :], xp_ref[rows, :])


def _prep_tile(xp_ref, xs_ref, mod_ref, ng_ref, h_ref, is_s):
    g = ng_ref[...]
    for s in range(NG):
        rows = slice(s * GR, (s + 1) * GR)
        x = _load_x(xp_ref, xs_ref, is_s, rows)
        ms = jnp.mean(x * x, axis=-1, keepdims=True)
        y = (x * lax.rsqrt(ms + EPS)) * g
        shift = mod_ref[s:s + 1, 0:D]
        scale = mod_ref[s:s + 1, D:2 * D]
        h_ref[rows, :] = (y * (1.0 + scale) + shift).astype(jnp.bfloat16)


def _residual_tile(xp_ref, xs_ref, mod_ref, y, op_ref, is_s, res_weight, final_g):
    for s in range(NG):
        rows = slice(s * GR, (s + 1) * GR)
        x = _load_x(xp_ref, xs_ref, is_s, rows)
        gate = mod_ref[s:s + 1, 2 * D:3 * D]
        if res_weight != 1.0:
            gate = res_weight * gate
        out = x + gate * y[rows, :]
        if final_g is not None:
            ms = jnp.mean(out * out, axis=-1, keepdims=True)
            out = (out * lax.rsqrt(ms + EPS)) * final_g
        op_ref[rows, :] = out


def _mod_kernel(c_ref, w_ref, b_ref, o_ref):
    c = c_ref[...]
    c_act = (c * jax.nn.sigmoid(c)).astype(jnp.bfloat16)
    o_ref[...] = _dot(c_act, w_ref[...].astype(jnp.bfloat16)) + b_ref[...]


def _mod_call(c2, ada_w, ada_b):
    n_chunk = 9 * D // D
    return pl.pallas_call(
        _mod_kernel,
        name="ada_mod",
        grid=(DEPTH, n_chunk),
        in_specs=[
            pl.BlockSpec((2 * NG, D), lambda l, j: (0, 0)),
            pl.BlockSpec((None, D, D), lambda l, j: (l, 0, j)),
            pl.BlockSpec((None, 1, D), lambda l, j: (l, 0, j)),
        ],
        out_specs=pl.BlockSpec((None, 2 * NG, D), lambda l, j: (l, 0, j)),
        out_shape=jax.ShapeDtypeStruct((DEPTH, 2 * NG, 9 * D), jnp.float32),
        compiler_params=pltpu.CompilerParams(
            dimension_semantics=("arbitrary", "arbitrary")),
    )(c2, ada_w, ada_b.reshape(DEPTH, 1, 9 * D))


def _resident(shape):
    nd = len(shape)
    return pl.BlockSpec(shape, lambda i: (0,) * nd, pipeline_mode=pl.Buffered(1))


def _x_specs():
    xp = pl.BlockSpec((TM, D), lambda i: (jnp.maximum(i - 1, 0), 0))
    xs = pl.BlockSpec((TM, D), lambda i: (0, 0), pipeline_mode=pl.Buffered(1))
    return xp, xs


def _mod_spec(layer, sub):
    return pl.BlockSpec((None, None, NG, 3 * D),
                        lambda i: (layer, jnp.where(i == 0, 1, 0), 0, sub))


def _y_specs():
    yp = pl.BlockSpec((TM, D), lambda i: (jnp.maximum(i - 1, 0), 0))
    ys = pl.BlockSpec((TM, D), lambda i: (0, 0))
    return yp, ys


_PARAMS = pltpu.CompilerParams(dimension_semantics=("arbitrary",),
                               vmem_limit_bytes=VMEM_LIMIT)


def _ffn_kernel(*refs, final):
    if final:
        (xp_ref, xs_ref, mod_ref, ng_ref, wgu_ref, wd_ref, fg_ref,
         op_ref, os_ref, h_ref, act_ref) = refs
        final_g = fg_ref[...]
    else:
        (xp_ref, xs_ref, mod_ref, ng_ref, wgu_ref, wd_ref,
         op_ref, os_ref, h_ref, act_ref) = refs
        final_g = None
    i = pl.program_id(0)
    is_s = i == 0

    _prep_tile(xp_ref, xs_ref, mod_ref, ng_ref, h_ref, is_s)

    c0 = 0
    while c0 < F:
        cw = min(FFN_CHUNK, F - c0)
        h = h_ref[...]
        g = _dot(h, wgu_ref[:, c0:c0 + cw])
        u = _dot(h, wgu_ref[:, F + c0:F + c0 + cw])
        act_ref[:, c0:c0 + cw] = (jax.nn.silu(g) * u).astype(jnp.bfloat16)
        c0 += cw

    y = _dot(act_ref[...], wd_ref[...])
    _residual_tile(xp_ref, xs_ref, mod_ref, y, op_ref, is_s, FFN_RES, final_g)

    @pl.when(is_s)
    def _():
        os_ref[...] = op_ref[...]


def _ffn_call(xp, xs, mod, layer, sub, ng, wgu, wd, final_g=None):
    n_tiles = xp.shape[0] // TM + 1
    xp_spec, xs_spec = _x_specs()
    in_specs = [xp_spec, xs_spec, _mod_spec(layer, sub), _resident((1, D)),
                _resident((D, 2 * F)), _resident((F, D))]
    args = [xp, xs, mod, ng, wgu, wd]
    if final_g is not None:
        in_specs.append(_resident((1, D)))
        args.append(final_g)
    return pl.pallas_call(
        functools.partial(_ffn_kernel, final=final_g is not None),
        name="ffn",
        grid=(n_tiles,),
        in_specs=in_specs,
        out_specs=list(_y_specs()),
        out_shape=[jax.ShapeDtypeStruct(xp.shape, jnp.float32),
                   jax.ShapeDtypeStruct(xs.shape, jnp.float32)],
        scratch_shapes=[pltpu.VMEM((TM, D), jnp.bfloat16),
                        pltpu.VMEM((TM, F), jnp.bfloat16)],
        compiler_params=_PARAMS,
    )(*args)


def _fill_history(buf_ref, car_ref, state_ref, val, hist, n_slab, is_s, first_prompt):
    for cb in range(n_slab):
        lanes = slice(cb * LANES, (cb + 1) * LANES)
        for s in range(NG):
            buf_ref[cb, s, hist:hist + GR, :] = val[s * GR:(s + 1) * GR, lanes]
            if s == 0:
                prev = jnp.where(first_prompt, 0.0, car_ref[cb])
            else:
                prev = val[s * GR - hist:s * GR, lanes]
            buf_ref[cb, s, 0:hist, :] = jnp.where(is_s, state_ref[s, :, lanes], prev)
        car_ref[cb] = val[TM - hist:TM, lanes]


def _mixer_ab_kernel(xp_ref, xs_ref, mod_ref, ng_ref, win_ref, cw_ref, cbias_ref,
                     lg_ref, lb_ref, wgrp_ref, bscale_ref, wout_ref, sa_ref, sb_ref,
                     op_ref, os_ref, alast_ref, asamp_ref, blast_ref, bsamp_ref,
                     h_ref, cbuf_ref, pbuf_ref, acar_ref, bcar_ref, d_ref, ab_ref):
    i = pl.program_id(0)
    is_s = i == 0
    first_prompt = i == 1

    _prep_tile(xp_ref, xs_ref, mod_ref, ng_ref, h_ref, is_s)

    p = _dot(h_ref[...], win_ref[...])
    a = p[:, 0:D_A] * jax.nn.sigmoid(p[:, D_A:2 * D_A])
    b_u = p[:, 2 * D_A:2 * D_A + D_B]
    alast_ref[...] = a
    blast_ref[...] = b_u

    n_slab = D_A // LANES
    _fill_history(cbuf_ref, acar_ref, sa_ref, a, HA, n_slab, is_s, first_prompt)
    _fill_history(pbuf_ref, bcar_ref, sb_ref, b_u, HB, n_slab, is_s, first_prompt)

    row = lax.broadcasted_iota(jnp.int32, (GR, LANES), 0)
    for s in range(NG):
        rows = slice(s * GR, (s + 1) * GR)
        conv = []
        for cb in range(n_slab):
            lanes = slice(cb * LANES, (cb + 1) * LANES)
            acc = cbuf_ref[cb, s, pl.ds(HA - (CONV_A - 1), GR), :] * cw_ref[0:1, lanes]
            for k in range(1, CONV_A):
                acc = acc + (cbuf_ref[cb, s, pl.ds(HA - (CONV_A - 1) + k, GR), :]
                             * cw_ref[k:k + 1, lanes])
            conv.append(acc)
        yc = jnp.concatenate(conv, axis=1) + cbias_ref[...]
        mu = jnp.mean(yc, axis=-1, keepdims=True)
        xc = yc - mu
        var = jnp.mean(xc * xc, axis=-1, keepdims=True)
        yn = (xc * lax.rsqrt(var + EPS)) * lg_ref[...] + lb_ref[...]
        ab_ref[rows, 0:D_A] = jax.nn.silu(yn).astype(jnp.bfloat16)

        base = jnp.where(is_s, PAST_LEN, (i - 1) * TM + s * GR)
        pos1 = row + (base + 1)
        for gi, w in enumerate(POOL_WINDOWS):
            lanes = slice(gi * LANES, (gi + 1) * LANES)
            tot = pbuf_ref[gi, s, pl.ds(HB, GR), :]
            for j in range(1, w):
                tot = tot + pbuf_ref[gi, s, pl.ds(HB - j, GR), :]
            cnt = jnp.minimum(pos1, w).astype(jnp.float32)
            d = tot / cnt - pbuf_ref[gi, s, pl.ds(HB, GR), :]
            d_ref[rows, lanes] = d.astype(jnp.bfloat16)

    for gi in range(len(POOL_WINDOWS)):
        lanes = slice(gi * LANES, (gi + 1) * LANES)
        bo = _dot(d_ref[:, lanes], wgrp_ref[gi]) * bscale_ref[:, lanes]
        ab_ref[:, D_A + gi * LANES:D_A + (gi + 1) * LANES] = bo.astype(jnp.bfloat16)

    y = _dot(ab_ref[...], wout_ref[...])
    _residual_tile(xp_ref, xs_ref, mod_ref, y, op_ref, is_s, 1.0, None)

    @pl.when(is_s)
    def _():
        os_ref[...] = op_ref[...]
        asamp_ref[...] = alast_ref[...]
        bsamp_ref[...] = blast_ref[...]


def _mixer_ab_call(xp, xs, mod, layer, ng, win, cw, cbias, lg, lb, wgrp, bscale, wout,
                   sa, sb):
    n_tiles = xp.shape[0] // TM + 1
    n_slab = D_A // LANES
    xp_spec, xs_spec = _x_specs()
    yp_spec, ys_spec = _y_specs()
    half = lambda: pl.BlockSpec((TM, D_A), lambda i: (0, 0))
    in_specs = [xp_spec, xs_spec, _mod_spec(layer, 1), _resident((1, D)),
                _resident((D, 2 * D_A + D_B)), _resident((CONV_A, D_A)),
                _resident((1, D_A)), _resident((1, D_A)), _resident((1, D_A)),
                _resident((len(POOL_WINDOWS), LANES, LANES)), _resident((1, D_B)),
                _resident((D_A + D_B, D)), _resident((NG, HA, D_A)),
                _resident((NG, HB, D_B))]
    return pl.pallas_call(
        _mixer_ab_kernel,
        name="mixer_ab",
        grid=(n_tiles,),
        in_specs=in_specs,
        out_specs=[yp_spec, ys_spec, half(), half(), half(), half()],
        out_shape=[jax.ShapeDtypeStruct(xp.shape, jnp.float32),
                   jax.ShapeDtypeStruct(xs.shape, jnp.float32)]
                  + [jax.ShapeDtypeStruct((TM, D_A), jnp.float32)] * 4,
        scratch_shapes=[pltpu.VMEM((TM, D), jnp.bfloat16),
                        pltpu.VMEM((n_slab, NG, HA + GR, LANES), jnp.float32),
                        pltpu.VMEM((n_slab, NG, HB + GR, LANES), jnp.float32),
                        pltpu.VMEM((n_slab, HA, LANES), jnp.float32),
                        pltpu.VMEM((n_slab, HB, LANES), jnp.float32),
                        pltpu.VMEM((TM, D_B), jnp.bfloat16),
                        pltpu.VMEM((TM, D_A + D_B), jnp.bfloat16)],
        compiler_params=_PARAMS,
    )(xp, xs, mod, ng, win, cw, cbias, lg, lb, wgrp, bscale, wout, sa, sb)


def _mixer_c_kernel(xp_ref, xs_ref, mod_ref, ng_ref, win_ref, cw_ref, wout_ref, sc_ref,
                    op_ref, os_ref, zlast_ref, zsamp_ref,
                    h_ref, zbuf_ref, zcar_ref, q_ref):
    i = pl.program_id(0)
    is_s = i == 0
    first_prompt = i == 1

    _prep_tile(xp_ref, xs_ref, mod_ref, ng_ref, h_ref, is_s)

    p = _dot(h_ref[...], win_ref[...])
    bg = p[:, 0:D]
    z = p[:, D:2 * D] * p[:, 2 * D:3 * D]
    zlast_ref[...] = z

    n_slab = D // LANES
    _fill_history(zbuf_ref, zcar_ref, sc_ref, z, HC, n_slab, is_s, first_prompt)

    for s in range(NG):
        rows = slice(s * GR, (s + 1) * GR)
        for cb in range(n_slab):
            lanes = slice(cb * LANES, (cb + 1) * LANES)
            acc = zbuf_ref[cb, s, pl.ds(HC - (CONV_C - 1), GR), :] * cw_ref[0:1, lanes]
            for k in range(1, CONV_C):
                acc = acc + (zbuf_ref[cb, s, pl.ds(HC - (CONV_C - 1) + k, GR), :]
                             * cw_ref[k:k + 1, lanes])
            q_ref[rows, lanes] = (bg[rows, lanes] * acc).astype(jnp.bfloat16)

    y = _dot(q_ref[...], wout_ref[...])
    _residual_tile(xp_ref, xs_ref, mod_ref, y, op_ref, is_s, 1.0, None)

    @pl.when(is_s)
    def _():
        os_ref[...] = op_ref[...]
        zsamp_ref[...] = zlast_ref[...]


def _mixer_c_call(xp, xs, mod, layer, ng, win, cw, wout, sc):
    n_tiles = xp.shape[0] // TM + 1
    n_slab = D // LANES
    xp_spec, xs_spec = _x_specs()
    yp_spec, ys_spec = _y_specs()
    full = lambda: pl.BlockSpec((TM, D), lambda i: (0, 0))
    in_specs = [xp_spec, xs_spec, _mod_spec(layer, 1), _resident((1, D)),
                _resident((D, 3 * D)), _resident((CONV_C, D)), _resident((D, D)),
                _resident((NG, HC, D))]
    return pl.pallas_call(
        _mixer_c_kernel,
        name="mixer_c",
        grid=(n_tiles,),
        in_specs=in_specs,
        out_specs=[yp_spec, ys_spec, full(), full()],
        out_shape=[jax.ShapeDtypeStruct(xp.shape, jnp.float32),
                   jax.ShapeDtypeStruct(xs.shape, jnp.float32)]
                  + [jax.ShapeDtypeStruct((TM, D), jnp.float32)] * 2,
        scratch_shapes=[pltpu.VMEM((TM, D), jnp.bfloat16),
                        pltpu.VMEM((n_slab, NG, HC + GR, LANES), jnp.float32),
                        pltpu.VMEM((n_slab, HC, LANES), jnp.float32),
                        pltpu.VMEM((TM, D), jnp.bfloat16)],
        compiler_params=_PARAMS,
    )(xp, xs, mod, ng, win, cw, wout, sc)


def _pad_front(state, rows):
    pad = rows - state.shape[-2]
    return jnp.pad(state, ((0, 0), (0, 0), (pad, 0), (0, 0)))


def kernel(x_prompt, x_sample, state_conv_a, state_pool_b, state_conv_c, c_prompt, c_sample, ada_w, ada_b, norm_g, ffn_w_gu, ffn_w_down, ab_w_in, a_conv_w, a_conv_b, a_ln_g, a_ln_b, b_w_group, b_scale, ab_w_out, c_w_in, c_conv_w, c_w_out, final_g):
    batch, seq, _ = x_prompt.shape
    n_stream, n_new, _ = x_sample.shape
    assert batch == 1 and seq % TM == 0 and n_stream == NG and n_new == GR
    bf16 = jnp.bfloat16

    xp = x_prompt.reshape(seq, D)
    xs = x_sample.reshape(TM, D)

    c2 = jnp.concatenate([jnp.broadcast_to(c_prompt, (NG, D)), c_sample], axis=0)
    mod = _mod_call(c2, ada_w, ada_b).reshape(DEPTH, 2, NG, 9 * D)

    sa = _pad_front(state_conv_a, HA)
    sb = _pad_front(state_pool_b, HB)
    sc = _pad_front(state_conv_c, HC)

    wgu = ffn_w_gu.astype(bf16)
    wd = ffn_w_down.astype(bf16)
    ab_in = ab_w_in.astype(bf16)
    ab_out = ab_w_out.astype(bf16)
    wgrp = b_w_group.astype(bf16)
    c_in = c_w_in.astype(bf16)
    c_out = c_w_out.astype(bf16)

    new_a_p, new_b_p, new_c_p, new_a_s, new_b_s, new_c_s = [], [], [], [], [], []
    for l in range(DEPTH):
        ng = norm_g[l].reshape(3, 1, D)
        xp, xs = _ffn_call(xp, xs, mod, l, 0, ng[0], wgu[l, 0], wd[l, 0])
        if l % 2 == 0:
            e = l // 2
            xp, xs, a_last, a_samp, b_last, b_samp = _mixer_ab_call(
                xp, xs, mod, l, ng[1], ab_in[e], a_conv_w[e], a_conv_b[e].reshape(1, D_A),
                a_ln_g[e].reshape(1, D_A), a_ln_b[e].reshape(1, D_A), wgrp[e],
                b_scale[e].reshape(1, D_B), ab_out[e], sa[e], sb[e])
            new_a_p.append(a_last[None, TM - (CONV_A - 1):])
            new_b_p.append(b_last[None, TM - (HB - 1):])
            new_a_s.append(a_samp.reshape(NG, GR, D_A)[:, GR - (CONV_A - 1):])
            new_b_s.append(b_samp.reshape(NG, GR, D_B)[:, GR - (HB - 1):])
        else:
            o = l // 2
            xp, xs, z_last, z_samp = _mixer_c_call(
                xp, xs, mod, l, ng[1], c_in[o], c_conv_w[o], c_out[o], sc[o])
            new_c_p.append(z_last[None, TM - (CONV_C - 1):])
            new_c_s.append(z_samp.reshape(NG, GR, D)[:, GR - (CONV_C - 1):])
        fg = final_g.reshape(1, D) if l == DEPTH - 1 else None
        xp, xs = _ffn_call(xp, xs, mod, l, 2, ng[2], wgu[l, 1], wd[l, 1], fg)

    return (xp.reshape(1, seq, D), xs.reshape(NG, GR, D),
            jnp.stack(new_a_p), jnp.stack(new_b_p), jnp.stack(new_c_p),
            jnp.stack(new_a_s), jnp.stack(new_b_s), jnp.stack(new_c_s))
```
